```python
import jax, jax.numpy as jnp
from jax import lax
import numpy as np

D_MODEL = 1024
BATCH = 16
SEQ = 4096
DEPTH = 4

CHUNK = 64
N_MIXERS = 2
POOL_WINDOWS = (2, 4, 8, 16)
N_POOL_GROUPS = len(POOL_WINDOWS)
POOL_GROUP_DIM = D_MODEL // N_POOL_GROUPS
SGU_BLOCK = 128
SGU_GROUPS = 4
SGU_DIM = D_MODEL
SGU_GROUP_DIM = SGU_DIM // SGU_GROUPS
N_MEM = 256
MEM_HEADS = 4
MEM_HEAD_DIM = D_MODEL // MEM_HEADS
D_FF = 4 * D_MODEL
EPS = 1e-6
N_A_LAYERS = (DEPTH + 1) // 2
N_B_LAYERS = DEPTH // 2

kernel_name = "hybrid_pool_sgu_memory_trunk"


def rmsnorm(x, g):
    xf = x.astype(jnp.float32)
    y = xf * lax.rsqrt(jnp.mean(xf * xf, axis=-1, keepdims=True) + EPS)
    return (y * g.astype(jnp.float32)).astype(x.dtype)


def layernorm(x, g, b):
    xf = x.astype(jnp.float32)
    mu = jnp.mean(xf, axis=-1, keepdims=True)
    var = jnp.mean(jnp.square(xf - mu), axis=-1, keepdims=True)
    y = (xf - mu) * lax.rsqrt(var + EPS)
    return (y * g.astype(jnp.float32) + b.astype(jnp.float32)).astype(x.dtype)


def pool_mixer(h, w_grp, scale):
    B, S, D = h.shape
    hf = h.astype(jnp.float32)
    cs = jnp.pad(jnp.cumsum(hf, axis=1), ((0, 0), (1, 0), (0, 0)))
    csg = cs.reshape(B, S + 1, N_POOL_GROUPS, POOL_GROUP_DIM)
    hg = hf.reshape(B, S, N_POOL_GROUPS, POOL_GROUP_DIM)
    t = jnp.arange(S)
    outs = []
    for g, w in enumerate(POOL_WINDOWS):
        c = csg[:, :, g]
        lo = jnp.maximum(t + 1 - w, 0)
        count = jnp.minimum(t + 1, w).astype(jnp.float32)
        mean = (c[:, t + 1] - c[:, lo]) / count[None, :, None]
        outs.append(mean - hg[:, :, g])
    pooled = jnp.stack(outs, axis=2)
    mixed = jnp.einsum('bsgc,gcd->bsgd', pooled, w_grp.astype(jnp.float32)).reshape(B, S, D)
    return (mixed * scale.astype(jnp.float32)).astype(h.dtype)


def sgu_mask():
    p = jnp.arange(SGU_BLOCK)
    return (p[:, None] // CHUNK) >= (p[None, :] // CHUNK)


def sgu_mixer(h, w_in, ln_g, ln_b, w_s, b_s, w_out):
    B, S, D = h.shape
    z = jax.nn.gelu(h @ w_in)
    u, v = jnp.split(z, 2, axis=-1)
    v = layernorm(v, ln_g, ln_b)
    nblk = S // SGU_BLOCK
    v = v.reshape(B, nblk, SGU_BLOCK, SGU_GROUPS, SGU_GROUP_DIM)
    ws = jnp.where(sgu_mask()[None], w_s, jnp.zeros_like(w_s))
    s = jnp.einsum('gpq,bnqgc->bnpgc', ws, v) + b_s.T[None, None, :, :, None]
    s = s.reshape(B, S, SGU_DIM)
    return (u * s) @ w_out


def mem_attention(h, m, w_q, w_kv, w_o):
    B, S, D = h.shape
    q = (h @ w_q).reshape(B, S, MEM_HEADS, MEM_HEAD_DIM)
    k, v = jnp.split(m @ w_kv, 2, axis=-1)
    k = k.reshape(B, N_MEM, MEM_HEADS, MEM_HEAD_DIM)
    v = v.reshape(B, N_MEM, MEM_HEADS, MEM_HEAD_DIM)
    scores = jnp.einsum('bshd,bnhd->bhsn', q, k).astype(jnp.float32) * (MEM_HEAD_DIM ** -0.5)
    p = jax.nn.softmax(scores, axis=-1).astype(v.dtype)
    o = jnp.einsum('bhsn,bnhd->bshd', p, v).reshape(B, S, D)
    return o @ w_o


def squared_relu_mlp(h, w1, w2):
    return jnp.square(jax.nn.relu(h @ w1)) @ w2


def setup_inputs(seed: int = 0) -> dict:
    key = jax.random.key(seed)
    ks = jax.random.split(key, 24)
    f32 = jnp.float32

    def nrm(k, shape, scale):
        return jax.random.normal(k, shape, f32) * scale

    def gain(k, shape):
        return 1.0 + 0.05 * jax.random.normal(k, shape, f32)

    out_scale = 0.5
    return {
        "x": jax.random.normal(ks[0], (BATCH, SEQ, D_MODEL), f32),
        "mem": jax.random.normal(ks[1], (BATCH, N_MEM, D_MODEL), f32),
        "norm_mix_g": gain(ks[2], (DEPTH, D_MODEL)),
        "norm_mem_g": gain(ks[3], (DEPTH, D_MODEL)),
        "norm_memkv_g": gain(ks[4], (DEPTH, D_MODEL)),
        "norm_ffn_g": gain(ks[5], (DEPTH, D_MODEL)),
        "final_norm_g": gain(ks[6], (D_MODEL,)),
        "pool_w": nrm(ks[7], (N_A_LAYERS, N_POOL_GROUPS, POOL_GROUP_DIM, POOL_GROUP_DIM), POOL_GROUP_DIM ** -0.5),
        "pool_scale": 0.5 + 0.05 * jax.random.normal(ks[8], (N_A_LAYERS, D_MODEL), f32),
        "sgu_w_in": nrm(ks[9], (N_B_LAYERS, D_MODEL, 2 * SGU_DIM), D_MODEL ** -0.5),
        "sgu_ln_g": gain(ks[10], (N_B_LAYERS, SGU_DIM)),
        "sgu_ln_b": nrm(ks[11], (N_B_LAYERS, SGU_DIM), 0.02),
        "sgu_w_s": nrm(ks[12], (N_B_LAYERS, SGU_GROUPS, SGU_BLOCK, SGU_BLOCK), 0.5 * SGU_BLOCK ** -0.5),
        "sgu_b_s": gain(ks[13], (N_B_LAYERS, SGU_GROUPS, SGU_BLOCK)),
        "sgu_w_out": nrm(ks[14], (N_B_LAYERS, SGU_DIM, D_MODEL), out_scale * SGU_DIM ** -0.5),
        "mem_w_q": nrm(ks[15], (DEPTH, D_MODEL, D_MODEL), D_MODEL ** -0.5),
        "mem_w_kv": nrm(ks[16], (DEPTH, D_MODEL, 2 * D_MODEL), D_MODEL ** -0.5),
        "mem_w_o": nrm(ks[17], (DEPTH, D_MODEL, D_MODEL), out_scale * D_MODEL ** -0.5),
        "ffn_w1": nrm(ks[18], (DEPTH, D_MODEL, D_FF), D_MODEL ** -0.5),
        "ffn_w2": nrm(ks[19], (DEPTH, D_FF, D_MODEL), out_scale * D_FF ** -0.5),
    }


def reference(x, mem, norm_mix_g, norm_mem_g, norm_memkv_g, norm_ffn_g, final_norm_g,
              pool_w, pool_scale, sgu_w_in, sgu_ln_g, sgu_ln_b, sgu_w_s, sgu_b_s, sgu_w_out,
              mem_w_q, mem_w_kv, mem_w_o, ffn_w1, ffn_w2):
    for i in range(DEPTH):
        h = rmsnorm(x, norm_mix_g[i])
        j = i // N_MIXERS
        if i % N_MIXERS == 0:
            x = x + pool_mixer(h, pool_w[j], pool_scale[j])
        else:
            x = x + sgu_mixer(h, sgu_w_in[j], sgu_ln_g[j], sgu_ln_b[j],
                              sgu_w_s[j], sgu_b_s[j], sgu_w_out[j])
        h = rmsnorm(x, norm_mem_g[i])
        m = rmsnorm(mem, norm_memkv_g[i])
        x = x + mem_attention(h, m, mem_w_q[i], mem_w_kv[i], mem_w_o[i])
        h = rmsnorm(x, norm_ffn_g[i])
        x = x + squared_relu_mlp(h, ffn_w1[i], ffn_w2[i])
    return rmsnorm(x, final_norm_g)
```

```python
import functools

import jax
import jax.numpy as jnp
from jax import lax
from jax.experimental import pallas as pl
from jax.experimental.pallas import tpu as pltpu

EPS = 1e-6
CHUNK = 64
POOL_WINDOWS = (2, 4, 8, 16)
SGU_BLOCK = 128
SGU_GROUPS = 4
MEM_HEADS = 4

HALO = max(POOL_WINDOWS)
TOKEN_TILE = 512
FFN_CHUNK = 1024
VMEM_LIMIT_BYTES = 56 * 1024 * 1024

BF16 = jnp.bfloat16
F32 = jnp.float32


def _rms(x, g):
    ms = jnp.mean(x * x, axis=-1, keepdims=True)
    return x * lax.rsqrt(ms + EPS) * g


def _dot(a, b):
    return jnp.dot(a, b, preferred_element_type=F32)


def _dot_nt(a, b):
    return lax.dot_general(a, b, (((1,), (1,)), ((), ())), preferred_element_type=F32)


def _kv_kernel(mem_ref, g_ref, wkv_ref, k_ref, v_ref):
    d = mem_ref.shape[-1]
    m = _rms(mem_ref[...], g_ref[...]).astype(BF16)
    kv = _dot(m, wkv_ref[...])
    k_ref[...] = kv[:, :d].astype(BF16)
    v_ref[...] = kv[:, d:].astype(BF16)


def _kv_proj(mem, g_kv, w_kv):
    depth, d = g_kv.shape
    b, n_mem, _ = mem.shape
    out = jax.ShapeDtypeStruct((depth, b, n_mem, d), BF16)
    return pl.pallas_call(
        _kv_kernel,
        grid=(depth, b),
        in_specs=[
            pl.BlockSpec((None, n_mem, d), lambda i, j: (j, 0, 0)),
            pl.BlockSpec((None, 1, d), lambda i, j: (i, 0, 0)),
            pl.BlockSpec((None, d, 2 * d), lambda i, j: (i, 0, 0)),
        ],
        out_specs=[
            pl.BlockSpec((None, None, n_mem, d), lambda i, j: (i, j, 0, 0)),
            pl.BlockSpec((None, None, n_mem, d), lambda i, j: (i, j, 0, 0)),
        ],
        out_shape=[out, out],
        compiler_params=pltpu.CompilerParams(
            dimension_semantics=("arbitrary", "arbitrary"),
            vmem_limit_bytes=VMEM_LIMIT_BYTES),
        name="kv_proj",
    )(mem, g_kv.reshape(depth, 1, d), w_kv)


def _mem_attention(x, g, wq_ref, k_ref, v_ref, wo_ref):
    d = x.shape[-1]
    hd = d // MEM_HEADS
    h = _rms(x, g).astype(BF16)
    q = _dot(h, wq_ref[...])
    heads = []
    for i in range(MEM_HEADS):
        cols = slice(i * hd, (i + 1) * hd)
        s = _dot_nt(q[:, cols].astype(BF16), k_ref[:, cols]) * (hd ** -0.5)
        e = jnp.exp(s - jnp.max(s, axis=-1, keepdims=True))
        p = e * (1.0 / jnp.sum(e, axis=-1, keepdims=True))
        heads.append(_dot(p.astype(BF16), v_ref[:, cols]))
    o = jnp.concatenate(heads, axis=-1).astype(BF16)
    return x + _dot(o, wo_ref[...])


def _relu2_mlp(x, g, w1_ref, w2_ref):
    d_ff = w1_ref.shape[-1]
    h = _rms(x, g).astype(BF16)
    acc = x
    for c in range(0, d_ff, FFN_CHUNK):
        a = jnp.maximum(_dot(h, w1_ref[:, c:c + FFN_CHUNK]), 0.0)
        acc = acc + _dot((a * a).astype(BF16), w2_ref[c:c + FFN_CHUNK, :])
    return acc


def _tail(x, gains_ref, wq_ref, k_ref, v_ref, wo_ref, w1_ref, w2_ref, o_ref, final):
    x = _mem_attention(x, gains_ref[1:2, :], wq_ref, k_ref, v_ref, wo_ref)
    x = _relu2_mlp(x, gains_ref[2:3, :], w1_ref, w2_ref)
    if final:
        x = _rms(x, gains_ref[3:4, :])
    o_ref[...] = x


def _pool_layer_kernel(x_ref, halo_ref, gains_ref, pw_ref, ps_ref,
                       wq_ref, k_ref, v_ref, wo_ref, w1_ref, w2_ref,
                       o_ref, hext_ref, *, final):
    t_tile, d = x_ref.shape
    gd = d // len(POOL_WINDOWS)
    g_mix = gains_ref[0:1, :]
    x = x_ref[...]
    h = _rms(x, g_mix)
    first = pl.program_id(1) == 0
    hext_ref[0:HALO, :] = jnp.where(first, 0.0, _rms(halo_ref[...], g_mix))
    hext_ref[HALO:HALO + t_tile, :] = h
    pos = pl.program_id(1) * t_tile + lax.broadcasted_iota(jnp.int32, (t_tile, 1), 0)
    mixed = []
    for gi, w in enumerate(POOL_WINDOWS):
        cols = slice(gi * gd, (gi + 1) * gd)
        acc = h[:, cols]
        for j in range(1, w):
            acc = acc + hext_ref[HALO - j:HALO - j + t_tile, cols]
        count = jnp.minimum(pos + 1, w).astype(F32)
        pooled = acc * (1.0 / count) - h[:, cols]
        mixed.append(_dot(pooled.astype(BF16), pw_ref[gi]))
    x = x + jnp.concatenate(mixed, axis=-1) * ps_ref[...]
    _tail(x, gains_ref, wq_ref, k_ref, v_ref, wo_ref, w1_ref, w2_ref, o_ref, final)


def _sgu_layer_kernel(x_ref, gains_ref, win_ref, lng_ref, lnb_ref, ws_ref, bs_ref, wout_ref,
                      wq_ref, k_ref, v_ref, wo_ref, w1_ref, w2_ref,
                      o_ref, *, final):
    t_tile, d = x_ref.shape
    gd = d // SGU_GROUPS
    x = x_ref[...]
    h = _rms(x, gains_ref[0:1, :]).astype(BF16)
    z = jax.nn.gelu(_dot(h, win_ref[...]))
    u = z[:, :d]
    v = z[:, d:]
    mu = jnp.mean(v, axis=-1, keepdims=True)
    vc = v - mu
    var = jnp.mean(vc * vc, axis=-1, keepdims=True)
    v = (vc * lax.rsqrt(var + EPS) * lng_ref[...] + lnb_ref[...]).astype(BF16)
    p_chunk = lax.broadcasted_iota(jnp.int32, (SGU_BLOCK, SGU_BLOCK), 0) // CHUNK
    q_chunk = lax.broadcasted_iota(jnp.int32, (SGU_BLOCK, SGU_BLOCK), 1) // CHUNK
    mask = p_chunk >= q_chunk
    groups = []
    for gi in range(SGU_GROUPS):
        cols = slice(gi * gd, (gi + 1) * gd)
        w = jnp.where(mask, ws_ref[gi], 0.0).astype(BF16)
        bias = bs_ref[:, gi:gi + 1]
        blocks = []
        for r in range(0, t_tile, SGU_BLOCK):
            blocks.append(_dot(w, v[r:r + SGU_BLOCK, cols]) + bias)
        groups.append(jnp.concatenate(blocks, axis=0))
    s = jnp.concatenate(groups, axis=-1)
    x = x + _dot((u * s).astype(BF16), wout_ref[...])
    _tail(x, gains_ref, wq_ref, k_ref, v_ref, wo_ref, w1_ref, w2_ref, o_ref, final)


def _resident(shape, index_map):
    return pl.BlockSpec(shape, index_map, pipeline_mode=pl.Buffered(1))


def _layer_call(i, x, gains, mixer_args, k, v, w_q, w_o, w1, w2, *, final):
    b, s, d = x.shape
    d_ff = w1.shape[-1]
    n_mem = k.shape[2]
    j = i // 2
    t_tile = TOKEN_TILE
    assert s % t_tile == 0 and t_tile % SGU_BLOCK == 0 and t_tile % HALO == 0
    assert d_ff % FFN_CHUNK == 0

    x_spec = pl.BlockSpec((None, t_tile, d), lambda bi, ti: (bi, ti, 0))
    gains_spec = _resident((None, 4, d), lambda bi, ti: (i, 0, 0))
    tail_specs = [
        _resident((None, d, d), lambda bi, ti: (i, 0, 0)),
        pl.BlockSpec((None, None, n_mem, d), lambda bi, ti: (i, bi, 0, 0)),
        pl.BlockSpec((None, None, n_mem, d), lambda bi, ti: (i, bi, 0, 0)),
        _resident((None, d, d), lambda bi, ti: (i, 0, 0)),
        _resident((None, d, d_ff), lambda bi, ti: (i, 0, 0)),
        _resident((None, d_ff, d), lambda bi, ti: (i, 0, 0)),
    ]
    tail_args = (w_q, k, v, w_o, w1, w2)

    if i % 2 == 0:
        pool_w, pool_scale = mixer_args
        ng, gd = pool_w.shape[1], pool_w.shape[2]
        rows_per_halo = t_tile // HALO
        halo_spec = pl.BlockSpec(
            (None, HALO, d), lambda bi, ti: (bi, jnp.maximum(ti * rows_per_halo - 1, 0), 0))
        in_specs = [x_spec, halo_spec, gains_spec,
                    _resident((None, ng, gd, gd), lambda bi, ti: (j, 0, 0, 0)),
                    _resident((None, 1, d), lambda bi, ti: (j, 0, 0))] + tail_specs
        args = (x, x, gains, pool_w, pool_scale) + tail_args
        body = functools.partial(_pool_layer_kernel, final=final)
        scratch = [pltpu.VMEM((HALO + t_tile, d), F32)]
        name = f"pool_layer_{i}"
    else:
        w_in, ln_g, ln_b, w_s, b_s_t, w_out = mixer_args
        in_specs = [x_spec, gains_spec,
                    _resident((None, d, 2 * d), lambda bi, ti: (j, 0, 0)),
                    _resident((None, 1, d), lambda bi, ti: (j, 0, 0)),
                    _resident((None, 1, d), lambda bi, ti: (j, 0, 0)),
                    _resident((None, SGU_GROUPS, SGU_BLOCK, SGU_BLOCK), lambda bi, ti: (j, 0, 0, 0)),
                    _resident((None, SGU_BLOCK, SGU_GROUPS), lambda bi, ti: (j, 0, 0)),
                    _resident((None, d, d), lambda bi, ti: (j, 0, 0))] + tail_specs
        args = (x, gains, w_in, ln_g, ln_b, w_s, b_s_t, w_out) + tail_args
        body = functools.partial(_sgu_layer_kernel, final=final)
        scratch = []
        name = f"sgu_layer_{i}"

    return pl.pallas_call(
        body,
        grid=(b, s // t_tile),
        in_specs=in_specs,
        out_specs=pl.BlockSpec((None, t_tile, d), lambda bi, ti: (bi, ti, 0)),
        out_shape=jax.ShapeDtypeStruct((b, s, d), x.dtype),
        scratch_shapes=scratch,
        compiler_params=pltpu.CompilerParams(
            dimension_semantics=("arbitrary", "arbitrary"),
            vmem_limit_bytes=VMEM_LIMIT_BYTES),
        name=name,
    )(*args)


def kernel(x, mem, norm_mix_g, norm_mem_g, norm_memkv_g, norm_ffn_g, final_norm_g,
           pool_w, pool_scale, sgu_w_in, sgu_ln_g, sgu_ln_b, sgu_w_s, sgu_b_s, sgu_w_out,
           mem_w_q, mem_w_kv, mem_w_o, ffn_w1, ffn_w2):
    depth, d = norm_mix_g.shape
    gains = jnp.stack([norm_mix_g, norm_mem_g, norm_ffn_g,
                       jnp.broadcast_to(final_norm_g, (depth, d))], axis=1)
    k, v = _kv_proj(mem, norm_memkv_g, mem_w_kv.astype(BF16))
    pool_args = (pool_w.astype(BF16), pool_scale[:, None, :])
    sgu_args = (sgu_w_in.astype(BF16), sgu_ln_g[:, None, :], sgu_ln_b[:, None, :],
                sgu_w_s, jnp.swapaxes(sgu_b_s, 1, 2), sgu_w_out.astype(BF16))
    w_q, w_o = mem_w_q.astype(BF16), mem_w_o.astype(BF16)
    w1, w2 = ffn_w1.astype(BF16), ffn_w2.astype(BF16)
    for i in range(depth):
        mixer_args = pool_args if i % 2 == 0 else sgu_args
        x = _layer_call(i, x, gains, mixer_args, k, v, w_q, w_o, w1, w2,
                        final=(i == depth - 1))
    return x
```
